```python
import jax, jax.numpy as jnp
from jax import lax
import numpy as np

D_MODEL = 2048
BATCH = 1
SEQ = 8192
DEPTH = 2

CHUNK = 64
N_GROUPS = 4
GROUP_W = D_MODEL // N_GROUPS
D_MIX = N_GROUPS * GROUP_W
SGU_BLOCK = 128
SGU_HEADS = 4
SGU_HEAD_DIM = GROUP_W // SGU_HEADS
CONV_WIDTH = 31
GLA_HEADS = 4
GLA_DK = GROUP_W // 2 // GLA_HEADS
GLA_DV = GROUP_W // GLA_HEADS
GLA_RANK = 16
GLA_GATE_NORM = 16.0
SB_HEADS = 4
SB_HEAD_DIM = GROUP_W // SB_HEADS
SB_QBLOCK = 128
N_EXPERTS = 32
TOP_K = 4
D_EXPERT = D_MODEL
SWIGLU_LIMIT = 7.0
SWIGLU_ALPHA = 1.702
MOE_BLOCK = 128
EPS = 1e-6

A_COLS = 2 * GROUP_W
B_COLS = 2 * GROUP_W
C_QK = GLA_HEADS * GLA_DK
C_COLS = 2 * C_QK + GROUP_W + GLA_RANK + GROUP_W
D_COLS = 3 * GROUP_W
OFF_B = A_COLS
OFF_C = OFF_B + B_COLS
OFF_D = OFF_C + C_COLS
D_IN = OFF_D + D_COLS

kernel_name = "hybrid_stream_encoder"


def rms_norm(x, g):
    xf = x.astype(jnp.float32)
    y = xf * lax.rsqrt(jnp.mean(xf * xf, axis=-1, keepdims=True) + EPS)
    return (y * g.astype(jnp.float32)).astype(x.dtype)


def layer_norm(x, g, b):
    xf = x.astype(jnp.float32)
    mu = jnp.mean(xf, axis=-1, keepdims=True)
    xc = xf - mu
    y = xc * lax.rsqrt(jnp.mean(xc * xc, axis=-1, keepdims=True) + EPS)
    return (y * g.astype(jnp.float32) + b.astype(jnp.float32)).astype(x.dtype)


def spatial_gating(z, ln_g, ln_b, w_s, b_s):
    bsz, seq, _ = z.shape
    u, v = jnp.split(jax.nn.gelu(z), 2, axis=-1)
    v = layer_norm(v, ln_g, ln_b).reshape(bsz, seq // SGU_BLOCK, SGU_BLOCK, SGU_HEADS, SGU_HEAD_DIM)
    pos = jnp.arange(SGU_BLOCK)
    mask = (pos[None, :] // CHUNK) <= (pos[:, None] // CHUNK)
    w = jnp.where(mask[None], w_s, 0.0).astype(v.dtype)
    s_out = jnp.einsum('hts,bnshd->bnthd', w, v) + b_s.T[None, None, :, :, None].astype(v.dtype)
    return u * s_out.reshape(bsz, seq, GROUP_W)


def conv_module(z, dw_w, dw_b, ln_g, ln_b, pw_w, pw_b):
    a, gt = jnp.split(z, 2, axis=-1)
    h = a * jax.nn.sigmoid(gt)
    h = lax.conv_general_dilated(
        h, dw_w[:, None, :].astype(h.dtype), window_strides=(1,),
        padding=[(CONV_WIDTH - 1, 0)], dimension_numbers=('NWC', 'WIO', 'NWC'),
        feature_group_count=GROUP_W) + dw_b.astype(h.dtype)
    h = jax.nn.silu(layer_norm(h, ln_g, ln_b))
    return h @ pw_w + pw_b


def gated_linear_attention(z, w2, b2, norm_g):
    bsz, seq, _ = z.shape
    f32 = jnp.float32
    n_chunks = seq // CHUNK
    q, k, v, lr, g_out = jnp.split(z, [C_QK, 2 * C_QK, 2 * C_QK + GROUP_W, 2 * C_QK + GROUP_W + GLA_RANK], axis=-1)
    q = q.astype(f32).reshape(bsz, seq, GLA_HEADS, GLA_DK) * (GLA_DK ** -0.5)
    k = k.astype(f32).reshape(bsz, seq, GLA_HEADS, GLA_DK)
    v = v.astype(f32).reshape(bsz, seq, GLA_HEADS, GLA_DV)
    gk = jax.nn.log_sigmoid((lr @ w2 + b2).astype(f32)) / GLA_GATE_NORM
    gk = gk.reshape(bsz, seq, GLA_HEADS, GLA_DK)

    def to_chunks(t):
        return jnp.moveaxis(t.reshape(bsz, n_chunks, CHUNK, t.shape[2], t.shape[3]), 1, 0)

    pos = jnp.arange(CHUNK)
    causal = pos[None, :] <= pos[:, None]

    def step(state, inp):
        qc, kc, vc, gc = inp
        b = jnp.cumsum(gc, axis=1)
        diff = b[:, :, None] - b[:, None, :]
        decay = jnp.exp(jnp.where(causal[None, :, :, None, None], diff, -jnp.inf))
        attn = jnp.einsum('bthd,bshd,btshd->bhts', qc, kc, decay)
        o = (jnp.einsum('bhts,bshv->bthv', attn, vc)
             + jnp.einsum('bthd,bhdv->bthv', qc * jnp.exp(b), state))
        b_last = b[:, -1]
        state = (jnp.exp(b_last)[..., None] * state
                 + jnp.einsum('bshd,bshv->bhdv', kc * jnp.exp(b_last[:, None] - b), vc))
        return state, o

    state0 = jnp.zeros((bsz, GLA_HEADS, GLA_DK, GLA_DV), f32)
    _, o = lax.scan(step, state0, (to_chunks(q), to_chunks(k), to_chunks(v), to_chunks(gk)))
    o = jnp.moveaxis(o, 0, 1).reshape(bsz, seq, GLA_HEADS, GLA_DV)
    o = rms_norm(o, norm_g) * jax.nn.silu(g_out.astype(f32).reshape(bsz, seq, GLA_HEADS, GLA_DV))
    return o.reshape(bsz, seq, GROUP_W).astype(z.dtype)


def stick_breaking_attention(z):
    bsz, seq, _ = z.shape
    f32 = jnp.float32
    q, k, v = [t.astype(f32).reshape(bsz, seq, SB_HEADS, SB_HEAD_DIM) for t in jnp.split(z, 3, axis=-1)]
    scale = SB_HEAD_DIM ** -0.5
    n_blocks = seq // SB_QBLOCK
    q_blocks = jnp.moveaxis(q.reshape(bsz, n_blocks, SB_QBLOCK, SB_HEADS, SB_HEAD_DIM), 1, 0)
    key_pos = jnp.arange(seq)

    def block(args):
        qb, start = args
        logits = jnp.einsum('bthd,bshd->bhts', qb, k) * scale
        mask = key_pos[None, :] < (start + jnp.arange(SB_QBLOCK))[:, None]
        log_1m = jnp.where(mask, jax.nn.log_sigmoid(-logits), 0.0)
        rest = lax.cumsum(log_1m, axis=3, reverse=True) - log_1m
        w = jnp.where(mask, jnp.exp(jax.nn.log_sigmoid(logits) + rest), 0.0)
        return jnp.einsum('bhts,bshd->bthd', w, v)

    o = lax.map(block, (q_blocks, jnp.arange(n_blocks) * SB_QBLOCK))
    return jnp.moveaxis(o, 0, 1).reshape(bsz, seq, GROUP_W).astype(z.dtype)


def hybrid_mixer(h, w_in, sgu_ln_g, sgu_ln_b, sgu_w, sgu_b, conv_dw_w, conv_dw_b, conv_ln_g,
                 conv_ln_b, conv_pw_w, conv_pw_b, gla_w2, gla_b2, gla_norm_g, merge_g, w_out):
    bsz, seq, _ = h.shape
    zz = h @ w_in
    y_a = spatial_gating(zz[..., :OFF_B], sgu_ln_g, sgu_ln_b, sgu_w, sgu_b)
    y_b = conv_module(zz[..., OFF_B:OFF_C], conv_dw_w, conv_dw_b, conv_ln_g, conv_ln_b, conv_pw_w, conv_pw_b)
    y_c = gated_linear_attention(zz[..., OFF_C:OFF_D], gla_w2, gla_b2, gla_norm_g)
    y_d = stick_breaking_attention(zz[..., OFF_D:])
    y = jnp.concatenate([y_a, y_b, y_c, y_d], axis=-1).reshape(bsz, seq, N_GROUPS, GROUP_W)
    y = rms_norm(y, merge_g.reshape(N_GROUPS, GROUP_W)).reshape(bsz, seq, D_MIX)
    return y @ w_out


def moe_ffn(h, router_w, router_b, w1, b1, w2, b2):
    bsz, seq, d = h.shape
    xt = h.reshape(-1, d)
    n_tok = xt.shape[0]
    n_assign = n_tok * TOP_K
    logits = (xt @ router_w + router_b).astype(jnp.float32)
    top_logits, top_idx = lax.top_k(logits, TOP_K)
    top_w = jax.nn.softmax(top_logits, axis=-1)
    flat_e = top_idx.reshape(-1)
    flat_tok = jnp.arange(n_assign) // TOP_K
    order = jnp.argsort(flat_e)
    sorted_e = flat_e[order]
    counts = jnp.bincount(flat_e, length=N_EXPERTS)
    padded = (counts + MOE_BLOCK - 1) // MOE_BLOCK * MOE_BLOCK
    start = jnp.cumsum(counts) - counts
    padded_end = jnp.cumsum(padded)
    padded_start = padded_end - padded
    dest = padded_start[sorted_e] + jnp.arange(n_assign) - start[sorted_e]
    n_blocks = -(-n_assign // MOE_BLOCK) + N_EXPERTS
    n_rows = n_blocks * MOE_BLOCK
    row_tok = jnp.zeros((n_rows,), jnp.int32).at[dest].set(flat_tok[order])
    row_w = jnp.zeros((n_rows,), jnp.float32).at[dest].set(top_w.reshape(-1)[order])
    block_e = jnp.minimum(jnp.searchsorted(padded_end, jnp.arange(n_blocks) * MOE_BLOCK, side='right'),
                          N_EXPERTS - 1)
    xs = xt[row_tok].reshape(n_blocks, MOE_BLOCK, d)

    def expert_block(args):
        xb, e = args
        hid = xb @ w1[e] + b1[e]
        gate, lin = jnp.split(hid, 2, axis=-1)
        gate = jnp.minimum(gate, SWIGLU_LIMIT)
        lin = jnp.clip(lin, -SWIGLU_LIMIT, SWIGLU_LIMIT)
        act = gate * jax.nn.sigmoid(SWIGLU_ALPHA * gate) * (lin + 1.0)
        return act @ w2[e] + b2[e]

    y = lax.map(expert_block, (xs, block_e)).reshape(n_rows, d)
    out = jnp.zeros_like(xt).at[row_tok].add((y * row_w[:, None]).astype(xt.dtype))
    return out.reshape(bsz, seq, d)


def setup_inputs(seed: int = 0) -> dict:
    key = jax.random.key(seed)
    ks = jax.random.split(key, 32)
    L = DEPTH
    f32 = jnp.float32

    def nrm(k, shape, scale):
        return jax.random.normal(k, shape, f32) * scale

    def gain(k, shape):
        return 1.0 + nrm(k, shape, 0.02)

    return {
        'x': nrm(ks[0], (BATCH, SEQ, D_MODEL), 1.0),
        'c': nrm(ks[1], (BATCH, D_MODEL), 1.0),
        'ada_w': nrm(ks[2], (L, D_MODEL, 6 * D_MODEL), D_MODEL ** -0.5),
        'ada_b': nrm(ks[3], (L, 6 * D_MODEL), 0.02),
        'norm1_g': gain(ks[4], (L, D_MODEL)),
        'w_in': nrm(ks[5], (L, D_MODEL, D_IN), D_MODEL ** -0.5),
        'sgu_ln_g': gain(ks[6], (L, GROUP_W)),
        'sgu_ln_b': nrm(ks[7], (L, GROUP_W), 0.02),
        'sgu_w': nrm(ks[8], (L, SGU_HEADS, SGU_BLOCK, SGU_BLOCK), 0.5 * SGU_BLOCK ** -0.5),
        'sgu_b': gain(ks[9], (L, SGU_HEADS, SGU_BLOCK)),
        'conv_dw_w': nrm(ks[10], (L, CONV_WIDTH, GROUP_W), CONV_WIDTH ** -0.5),
        'conv_dw_b': nrm(ks[11], (L, GROUP_W), 0.02),
        'conv_ln_g': gain(ks[12], (L, GROUP_W)),
        'conv_ln_b': nrm(ks[13], (L, GROUP_W), 0.02),
        'conv_pw_w': nrm(ks[14], (L, GROUP_W, GROUP_W), GROUP_W ** -0.5),
        'conv_pw_b': nrm(ks[15], (L, GROUP_W), 0.02),
        'gla_w2': nrm(ks[16], (L, GLA_RANK, C_QK), GLA_RANK ** -0.5),
        'gla_b2': nrm(ks[17], (L, C_QK), 0.1),
        'gla_norm_g': gain(ks[18], (L, GLA_DV)),
        'merge_g': gain(ks[19], (L, D_MIX)),
        'w_out': nrm(ks[20], (L, D_MIX, D_MODEL), D_MIX ** -0.5),
        'norm2_g': gain(ks[21], (L, D_MODEL)),
        'router_w': nrm(ks[22], (L, D_MODEL, N_EXPERTS), D_MODEL ** -0.5),
        'router_b': nrm(ks[23], (L, N_EXPERTS), 0.01),
        'moe_w1': nrm(ks[24], (L, N_EXPERTS, D_MODEL, 2 * D_EXPERT), D_MODEL ** -0.5),
        'moe_b1': nrm(ks[25], (L, N_EXPERTS, 2 * D_EXPERT), 0.02),
        'moe_w2': nrm(ks[26], (L, N_EXPERTS, D_EXPERT, D_MODEL), D_EXPERT ** -0.5),
        'moe_b2': nrm(ks[27], (L, N_EXPERTS, D_MODEL), 0.02),
        'final_g': gain(ks[28], (D_MODEL,)),
    }


def reference(x, c, ada_w, ada_b, norm1_g, w_in, sgu_ln_g, sgu_ln_b, sgu_w, sgu_b, conv_dw_w,
              conv_dw_b, conv_ln_g, conv_ln_b, conv_pw_w, conv_pw_b, gla_w2, gla_b2, gla_norm_g,
              merge_g, w_out, norm2_g, router_w, router_b, moe_w1, moe_b1, moe_w2, moe_b2, final_g):
    for l in range(DEPTH):
        mod = jax.nn.silu(c) @ ada_w[l] + ada_b[l]
        shift1, scale1, gate1, shift2, scale2, gate2 = jnp.split(mod[:, None, :], 6, axis=-1)
        h = rms_norm(x, norm1_g[l]) * (1.0 + scale1) + shift1
        x = x + gate1 * hybrid_mixer(h, w_in[l], sgu_ln_g[l], sgu_ln_b[l], sgu_w[l], sgu_b[l],
                                     conv_dw_w[l], conv_dw_b[l], conv_ln_g[l], conv_ln_b[l],
                                     conv_pw_w[l], conv_pw_b[l], gla_w2[l], gla_b2[l],
                                     gla_norm_g[l], merge_g[l], w_out[l])
        h = rms_norm(x, norm2_g[l]) * (1.0 + scale2) + shift2
        x = x + gate2 * moe_ffn(h, router_w[l], router_b[l], moe_w1[l], moe_b1[l], moe_w2[l], moe_b2[l])
    return rms_norm(x, final_g)
```

```python
import functools

import numpy as np
import jax
import jax.numpy as jnp
from jax import lax
from jax.experimental import pallas as pl
from jax.experimental.pallas import tpu as pltpu

F32 = jnp.float32
BF16 = jnp.bfloat16

D_MODEL = 2048
GROUP_W = 512
N_GROUPS = 4
EPS = 1e-6
CHUNK = 64
SGU_BLOCK = 128
SGU_HEADS = 4
CONV_WIDTH = 31
GLA_HEADS = 4
GLA_DK = 64
GLA_DV = 128
GLA_RANK = 16
GLA_GATE_NORM = 16.0
C_QK = GLA_HEADS * GLA_DK
SB_HEADS = 4
SB_HEAD_DIM = 128
N_EXPERTS = 32
TOP_K = 4
D_EXPERT = 2048
SWIGLU_LIMIT = 7.0
SWIGLU_ALPHA = 1.702

LANES = 128
SUBLANES = 8
VMEM_BYTES_V7X = 64 * 1024 * 1024

ZA_W = 2 * GROUP_W
ZB_W = 2 * GROUP_W
ZC_LR_OFF = 2 * C_QK + 2 * GROUP_W
ZC_W = ZC_LR_OFF + LANES
ZD_W = 3 * GROUP_W


def _cparams(sem, vmem_mb=None):
    kw = dict(dimension_semantics=sem)
    if vmem_mb is not None:
        kw["vmem_limit_bytes"] = vmem_mb * 1024 * 1024
    return pltpu.CompilerParams(**kw)


def _const_spec(shape):
    nd = len(shape)
    return pl.BlockSpec(shape, lambda *_: (0,) * nd, pipeline_mode=pl.Buffered(1))


def _rms(x, g):
    return x * lax.rsqrt(jnp.mean(x * x, axis=-1, keepdims=True) + EPS) * g


def _layer_norm(x, g, b):
    mu = jnp.mean(x, axis=-1, keepdims=True)
    xc = x - mu
    return xc * lax.rsqrt(jnp.mean(xc * xc, axis=-1, keepdims=True) + EPS) * g + b


def _sigmoid(x):
    return 1.0 / (1.0 + jnp.exp(-x))


def _silu(x):
    return x * _sigmoid(x)


_MOD_TN = 1024
_MOD_ROWS = 64


def _mod_kernel(c_ref, w_ref, b_ref, o_ref):
    d = w_ref.shape[1]
    tn = w_ref.shape[2]

    def body(i, acc):
        r0 = pl.multiple_of(i * _MOD_ROWS, _MOD_ROWS)
        c = c_ref[pl.ds(r0, _MOD_ROWS), :]
        p = w_ref[0, pl.ds(r0, _MOD_ROWS), :] * _silu(c)
        return acc + p.reshape(_MOD_ROWS // SUBLANES, SUBLANES, tn).sum(axis=0)

    acc = lax.fori_loop(0, d // _MOD_ROWS, body, jnp.zeros((SUBLANES, tn), F32))
    o_ref[0] = jnp.sum(acc, axis=0, keepdims=True) + b_ref[0]


def _modulation(c, ada_w, ada_b):
    n_l, d, n = ada_w.shape
    return pl.pallas_call(
        _mod_kernel,
        grid=(n_l, n // _MOD_TN),
        in_specs=[
            pl.BlockSpec((d, 1), lambda l, j: (0, 0)),
            pl.BlockSpec((1, d, _MOD_TN), lambda l, j: (l, 0, j)),
            pl.BlockSpec((1, 1, _MOD_TN), lambda l, j: (l, 0, j)),
        ],
        out_specs=pl.BlockSpec((1, 1, _MOD_TN), lambda l, j: (l, 0, j)),
        out_shape=jax.ShapeDtypeStruct((n_l, 1, n), F32),
        compiler_params=_cparams(("arbitrary", "arbitrary"), 40),
        name="adaln_modulation",
    )(c.reshape(d, 1), ada_w, ada_b.reshape(n_l, 1, n))


_INPROJ_TM = 256
_INPROJ_CH = 512


def _inproj_kernel(x_ref, g_ref, sc_ref, sh_ref, w_ref, za_ref, zb_ref, zc_ref, zd_ref, h_scr):
    x = x_ref[...]
    h = _rms(x, g_ref[...]) * (1.0 + sc_ref[...]) + sh_ref[...]
    h_scr[...] = h.astype(BF16)
    off = 0
    for ref in (za_ref, zb_ref, zc_ref, zd_ref):
        n = ref.shape[1]
        for c0 in range(0, n, _INPROJ_CH):
            c1 = min(c0 + _INPROJ_CH, n)
            ref[:, c0:c1] = jnp.dot(h_scr[...], w_ref[:, off + c0:off + c1],
                                    preferred_element_type=F32).astype(BF16)
        off += n


def _in_projection(x, g, scale, shift, w_p):
    s, d = x.shape
    tm = _INPROJ_TM
    widths = (ZA_W, ZB_W, ZC_W, ZD_W)
    row = lambda i: (i, 0)
    return pl.pallas_call(
        _inproj_kernel,
        grid=(s // tm,),
        in_specs=[
            pl.BlockSpec((tm, d), row),
            _const_spec((1, d)), _const_spec((1, d)), _const_spec((1, d)),
            _const_spec(w_p.shape),
        ],
        out_specs=[pl.BlockSpec((tm, w), row) for w in widths],
        out_shape=[jax.ShapeDtypeStruct((s, w), BF16) for w in widths],
        scratch_shapes=[pltpu.VMEM((tm, d), BF16)],
        compiler_params=_cparams(("arbitrary",), 48),
        name="norm_in_projection",
    )(x, g, scale, shift, w_p)


_SGU_TA = 256
_GELU_C = float(np.sqrt(2.0 / np.pi))


def _gelu_tanh(x):
    return 0.5 * x * (1.0 + jnp.tanh(_GELU_C * (x + 0.044715 * (x * x * x))))


def _sgu_kernel(z_ref, lng_ref, lnb_ref, w_ref, bs_ref, mg_ref, o_ref, y_scr):
    ta = z_ref.shape[0]
    hd = GROUP_W // SGU_HEADS
    g = _gelu_tanh(z_ref[...].astype(F32))
    u = g[:, :GROUP_W]
    v = _layer_norm(g[:, GROUP_W:], lng_ref[...], lnb_ref[...]).astype(BF16)
    t_pos = lax.broadcasted_iota(jnp.int32, (SGU_BLOCK, SGU_BLOCK), 0)
    s_pos = lax.broadcasted_iota(jnp.int32, (SGU_BLOCK, SGU_BLOCK), 1)
    mask = (s_pos // CHUNK) <= (t_pos // CHUNK)
    for h in range(SGU_HEADS):
        wm = jnp.where(mask, w_ref[h], 0.0).astype(BF16)
        for blk in range(ta // SGU_BLOCK):
            r0 = blk * SGU_BLOCK
            vb = v[r0:r0 + SGU_BLOCK, h * hd:(h + 1) * hd]
            s_out = jnp.dot(wm, vb, preferred_element_type=F32) + bs_ref[h]
            y_scr[r0:r0 + SGU_BLOCK, h * hd:(h + 1) * hd] = u[r0:r0 + SGU_BLOCK, h * hd:(h + 1) * hd] * s_out
    o_ref[...] = _rms(y_scr[...], mg_ref[...]).astype(BF16)


def _spatial_gating(z_a, ln_g, ln_b, w_s, b_s, merge_g):
    s = z_a.shape[0]
    ta = _SGU_TA
    return pl.pallas_call(
        _sgu_kernel,
        grid=(s // ta,),
        in_specs=[
            pl.BlockSpec((ta, ZA_W), lambda i: (i, 0)),
            _const_spec((1, GROUP_W)), _const_spec((1, GROUP_W)),
            _const_spec(w_s.shape), _const_spec((SGU_HEADS, SGU_BLOCK, 1)),
            _const_spec((1, GROUP_W)),
        ],
        out_specs=pl.BlockSpec((ta, GROUP_W), lambda i: (i, 0)),
        out_shape=jax.ShapeDtypeStruct((s, GROUP_W), BF16),
        scratch_shapes=[pltpu.VMEM((ta, GROUP_W), F32)],
        compiler_params=_cparams(("arbitrary",)),
        name="mixer_spatial_gating",
    )(z_a, ln_g, ln_b, w_s, b_s.reshape(SGU_HEADS, SGU_BLOCK, 1), merge_g)


_CONV_TB = 512
_CONV_HALO = 32
_CONV_RCH = 64


def _glu(z):
    return z[:, :GROUP_W] * _sigmoid(z[:, GROUP_W:])


def _conv_kernel(zp_ref, z_ref, dww_ref, dwb_ref, lng_ref, lnb_ref, pww_ref, pwb_ref, mg_ref, o_ref,
                 h_scr, c_scr):
    tb = z_ref.shape[0]
    first = pl.program_id(0) == 0
    halo = _glu(zp_ref[...].astype(F32))
    h_scr[0:_CONV_HALO, :] = jnp.where(first, 0.0, halo)
    h_scr[_CONV_HALO:, :] = _glu(z_ref[...].astype(F32))
    lead = _CONV_HALO - (CONV_WIDTH - 1)
    for r0 in range(0, tb, _CONV_RCH):
        acc = jnp.zeros((_CONV_RCH, GROUP_W), F32)
        for k in range(CONV_WIDTH):
            acc = acc + dww_ref[k:k + 1, :] * h_scr[r0 + lead + k:r0 + lead + k + _CONV_RCH, :]
        c_scr[r0:r0 + _CONV_RCH, :] = acc
    hc = _silu(_layer_norm(c_scr[...] + dwb_ref[...], lng_ref[...], lnb_ref[...]))
    y = jnp.dot(hc.astype(BF16), pww_ref[...], preferred_element_type=F32) + pwb_ref[...]
    o_ref[...] = _rms(y, mg_ref[...]).astype(BF16)


def _conv_module(z_b, dw_w, dw_b, ln_g, ln_b, pw_w, pw_b, merge_g):
    s = z_b.shape[0]
    tb = _CONV_TB
    per = tb // _CONV_HALO
    vec = _const_spec((1, GROUP_W))
    return pl.pallas_call(
        _conv_kernel,
        grid=(s // tb,),
        in_specs=[
            pl.BlockSpec((_CONV_HALO, ZB_W), lambda i: (jnp.maximum(i * per - 1, 0), 0)),
            pl.BlockSpec((tb, ZB_W), lambda i: (i, 0)),
            _const_spec((CONV_WIDTH, GROUP_W)), vec, vec, vec,
            _const_spec((GROUP_W, GROUP_W)), vec, vec,
        ],
        out_specs=pl.BlockSpec((tb, GROUP_W), lambda i: (i, 0)),
        out_shape=jax.ShapeDtypeStruct((s, GROUP_W), BF16),
        scratch_shapes=[pltpu.VMEM((tb + _CONV_HALO, GROUP_W), F32), pltpu.VMEM((tb, GROUP_W), F32)],
        compiler_params=_cparams(("arbitrary",)),
        name="mixer_conv_module",
    )(z_b, z_b, dw_w, dw_b, ln_g, ln_b, pw_w.astype(BF16), pw_b, merge_g)


def _relayout_w_in(w_in):
    d = w_in.shape[0]
    off_c = ZA_W + ZB_W
    c_qkv = 2 * C_QK + GROUP_W
    off_d = off_c + c_qkv + GLA_RANK + GROUP_W
    return jnp.concatenate([
        w_in[:, :off_c + c_qkv],
        w_in[:, off_c + c_qkv + GLA_RANK:off_d],
        w_in[:, off_c + c_qkv:off_c + c_qkv + GLA_RANK],
        jnp.zeros((d, LANES - GLA_RANK), w_in.dtype),
        w_in[:, off_d:],
    ], axis=1).astype(BF16)


_GLA_TC = 512
_GLA_LEVELS = 6


def _gla_constants():
    c = CHUNK
    t = np.arange(c)
    heads = np.arange(GLA_HEADS)
    mats = [(t[None, :] <= t[:, None])]
    q_masks, k_masks, a_masks = [], [], []
    head_eq = np.repeat(np.repeat(np.eye(GLA_HEADS, dtype=bool), c, axis=0), GLA_DK, axis=1)
    m = c // 2
    while m >= 1:
        r = (t // (2 * m)) * 2 * m + m - 1
        lo, hi = np.minimum(t, r), np.maximum(t, r)
        mats.append((t[None, :] > lo[:, None]) & (t[None, :] <= hi[:, None]))
        upper = (t % (2 * m)) >= m
        q_masks.append(head_eq & np.tile(upper, GLA_HEADS)[:, None])
        k_masks.append(np.broadcast_to((~upper)[:, None], (c, C_QK)))
        a_masks.append(np.tile((t[:, None] // (2 * m)) == (t[None, :] // (2 * m)), (GLA_HEADS, 1)))
        m //= 2
    q_masks.append(head_eq)
    k_masks.append(np.ones((c, C_QK), bool))
    a_masks.append(np.tile(np.eye(c, dtype=bool), (GLA_HEADS, 1)))
    cmat = np.concatenate(mats, axis=0).astype(np.float32)
    cmat3 = np.concatenate([cmat] * 3, axis=1)
    return (jnp.asarray(cmat3, BF16), jnp.asarray(np.stack(q_masks), F32),
            jnp.asarray(np.stack(k_masks), F32), jnp.asarray(np.stack(a_masks), F32))


def _split3(x):
    hi = x.astype(BF16)
    r1 = x - hi.astype(F32)
    mid = r1.astype(BF16)
    lo = (r1 - mid.astype(F32)).astype(BF16)
    return hi, mid, lo


def _log_sigmoid(x):
    return jnp.minimum(x, 0.0) - jnp.log1p(jnp.exp(-jnp.abs(x)))


_NT = (((1,), (1,)), ((), ()))
_TN = (((0,), (0,)), ((), ()))


def _gla_kernel(z_ref, w2_ref, b2_ref, ng_ref, mg_ref, cm_ref, qm_ref, km_ref, am_ref, o_ref, s_scr):
    c = CHUNK
    n_lv = _GLA_LEVELS

    @pl.when(pl.program_id(0) == 0)
    def _():
        s_scr[...] = jnp.zeros_like(s_scr)

    ones_col = jnp.ones((3 * c, LANES), BF16)

    def chunk(ci, carry):
        r0 = pl.multiple_of(ci * c, c)
        rows = pl.ds(r0, c)
        qs = z_ref[rows, 0:C_QK].astype(F32) * (GLA_DK ** -0.5)
        k = z_ref[rows, C_QK:2 * C_QK].astype(F32)
        v = z_ref[rows, 2 * C_QK:2 * C_QK + GROUP_W]
        gout = z_ref[rows, 2 * C_QK + GROUP_W:ZC_LR_OFF].astype(F32)
        lr = z_ref[rows, ZC_LR_OFF:ZC_W]
        pre = jnp.dot(jnp.concatenate([lr, lr, lr], axis=1), w2_ref[...], preferred_element_type=F32) + b2_ref[...]
        gk = _log_sigmoid(pre) * (1.0 / GLA_GATE_NORM)
        gs = jnp.concatenate(_split3(gk), axis=0)
        ball = jnp.dot(cm_ref[...], gs, preferred_element_type=F32)
        b = ball[0:c]
        a_acc = jnp.zeros((GLA_HEADS * c, c), F32)
        for lv in range(n_lv + 1):
            if lv < n_lv:
                e = jnp.exp(ball[(lv + 1) * c:(lv + 2) * c])
                qe, ke = qs * e, k * e
            else:
                qe, ke = qs, k
            qst = (jnp.concatenate([qe] * GLA_HEADS, axis=0) * qm_ref[lv]).astype(BF16)
            kk = (ke * km_ref[lv]).astype(BF16)
            a = lax.dot_general(qst, kk, _NT, preferred_element_type=F32)
            a_acc = a_acc + a * am_ref[lv]
        qb = (jnp.concatenate([qs * jnp.exp(b)] * GLA_HEADS, axis=0) * qm_ref[n_lv]).astype(BF16)
        o_all = (jnp.dot(a_acc.astype(BF16), v, preferred_element_type=F32)
                 + jnp.dot(qb, s_scr[...].astype(BF16), preferred_element_type=F32))
        kd = (k * jnp.exp(b[c - 1:c, :] - b)).astype(BF16)
        upd = lax.dot_general(kd, v, _TN, preferred_element_type=F32)
        dcol = jnp.exp(lax.dot_general(gs, ones_col, _TN, preferred_element_type=F32))
        s_scr[...] = s_scr[...] * jnp.concatenate([dcol] * GLA_HEADS, axis=1) + upd
        outs = []
        for h in range(GLA_HEADS):
            oh = o_all[h * c:(h + 1) * c, h * GLA_DV:(h + 1) * GLA_DV]
            outs.append(_rms(oh, ng_ref[...]) * _silu(gout[:, h * GLA_DV:(h + 1) * GLA_DV]))
        y = jnp.concatenate(outs, axis=1)
        o_ref[rows, :] = _rms(y, mg_ref[...]).astype(BF16)
        return carry

    lax.fori_loop(0, z_ref.shape[0] // c, chunk, 0)


def _gla(z_c, w2, b2, norm_g, merge_g):
    s = z_c.shape[0]
    tc = _GLA_TC
    cm, qm, km, am = _gla_constants()
    w2p = jnp.zeros((LANES, C_QK), F32).at[:GLA_RANK].set(w2)
    w2s = jnp.concatenate(_split3(w2p), axis=0)
    return pl.pallas_call(
        _gla_kernel,
        grid=(s // tc,),
        in_specs=[
            pl.BlockSpec((tc, ZC_W), lambda i: (i, 0)),
            _const_spec(w2s.shape), _const_spec((1, C_QK)), _const_spec((1, GLA_DV)), _const_spec((1, GROUP_W)),
            _const_spec(cm.shape), _const_spec(qm.shape), _const_spec(km.shape), _const_spec(am.shape),
        ],
        out_specs=pl.BlockSpec((tc, GROUP_W), lambda i: (i, 0)),
        out_shape=jax.ShapeDtypeStruct((s, GROUP_W), BF16),
        scratch_shapes=[pltpu.VMEM((GLA_HEADS * GLA_DK, GLA_HEADS * GLA_DV), F32)],
        compiler_params=_cparams(("arbitrary",)),
        name="mixer_gated_linear_attention",
    )(z_c, w2s, b2, norm_g, merge_g, cm, qm, km, am)


_SB_TQ = 128
_SB_TK = 128
_SB_STOP = -110.0


def _sb_kernel(q_ref, k_ref, v_ref, mg_ref, o_ref, acc_scr, car_scr):
    i = pl.program_id(0)
    tq, tk, hd = _SB_TQ, _SB_TK, SB_HEAD_DIM
    scale = hd ** -0.5
    acc_scr[...] = jnp.zeros_like(acc_scr)
    car_scr[...] = jnp.zeros_like(car_scr)
    jj = lax.broadcasted_iota(jnp.int32, (2 * tk, tk), 0) % tk
    ss = lax.broadcasted_iota(jnp.int32, (2 * tk, tk), 1)
    suffix = jnp.where(jj > ss, 1.0, 0.0).astype(BF16)
    q_pos = i * tq + lax.broadcasted_iota(jnp.int32, (tq, tk), 0)
    col = lax.broadcasted_iota(jnp.int32, (tq, tk), 1)

    def cond(st):
        j, cmax = st
        return jnp.logical_and(j >= 0, cmax > _SB_STOP)

    def body(st):
        j, _ = st
        k0 = pl.multiple_of(j * tk, tk)
        valid = (k0 + col) < q_pos
        cmax = jnp.float32(-jnp.inf)
        for h in range(SB_HEADS):
            hs = slice(h * hd, (h + 1) * hd)
            l = lax.dot_general(q_ref[:, hs], k_ref[pl.ds(k0, tk), hs], _NT, preferred_element_type=F32) * scale
            sp = jnp.log1p(jnp.exp(-jnp.abs(l)))
            ls_pos = jnp.minimum(l, 0.0) - sp
            log_1m = jnp.where(valid, ls_pos - l, 0.0)
            hi = log_1m.astype(BF16)
            lo = (log_1m - hi.astype(F32)).astype(BF16)
            rest_in = jnp.dot(jnp.concatenate([hi, lo], axis=1), suffix, preferred_element_type=F32)
            car = car_scr[h]
            w = jnp.where(valid, jnp.exp(ls_pos + rest_in + car), 0.0)
            acc_scr[h] = acc_scr[h] + jnp.dot(w.astype(BF16), v_ref[pl.ds(k0, tk), hs], preferred_element_type=F32)
            car = car + (rest_in[:, 0:1] + log_1m[:, 0:1])
            car_scr[h] = car
            cmax = jnp.maximum(cmax, jnp.max(car))
        return j - 1, cmax

    lax.while_loop(cond, body, ((i + 1) * (tq // tk) - 1, jnp.float32(0.0)))
    y = jnp.concatenate([acc_scr[h] for h in range(SB_HEADS)], axis=1)
    o_ref[...] = _rms(y, mg_ref[...]).astype(BF16)


def _stick_breaking(z_d, merge_g):
    s = z_d.shape[0]
    tq = _SB_TQ
    kv = lambda c: pl.BlockSpec((s, GROUP_W), lambda i: (0, c), pipeline_mode=pl.Buffered(1))
    return pl.pallas_call(
        _sb_kernel,
        grid=(s // tq,),
        in_specs=[pl.BlockSpec((tq, GROUP_W), lambda i: (i, 0)), kv(1), kv(2), _const_spec((1, GROUP_W))],
        out_specs=pl.BlockSpec((tq, GROUP_W), lambda i: (i, 0)),
        out_shape=jax.ShapeDtypeStruct((s, GROUP_W), BF16),
        scratch_shapes=[pltpu.VMEM((SB_HEADS, tq, SB_HEAD_DIM), F32), pltpu.VMEM((SB_HEADS, tq, _SB_TK), F32)],
        compiler_params=_cparams(("arbitrary",), 40),
        name="mixer_stick_breaking",
    )(z_d, z_d, z_d, merge_g)


_OUT_TM = 256
_ROUTER_PAD_BIAS = -1e30


def _outproj_kernel(ya_ref, yb_ref, yc_ref, yd_ref, w_ref, x_ref, g1_ref, n2_ref, sc_ref, sh_ref, rw_ref, rb_ref,
                    x1_ref, h2_ref, idx_ref, tw_ref):
    m = jnp.zeros(x_ref.shape, F32)
    for g, y_ref in enumerate((ya_ref, yb_ref, yc_ref, yd_ref)):
        m = m + jnp.dot(y_ref[...], w_ref[g * GROUP_W:(g + 1) * GROUP_W, :], preferred_element_type=F32)
    x1 = x_ref[...] + g1_ref[...] * m
    x1_ref[...] = x1
    h2 = _rms(x1, n2_ref[...]) * (1.0 + sc_ref[...]) + sh_ref[...]
    h2_ref[...] = h2
    logits = jnp.dot(h2, rw_ref[...], preferred_element_type=F32, precision=lax.Precision.HIGHEST) + rb_ref[...]
    lane = lax.broadcasted_iota(jnp.int32, logits.shape, 1)
    vals, idxs = [], []
    for _ in range(TOP_K):
        mx = jnp.max(logits, axis=-1, keepdims=True)
        am = jnp.min(jnp.where(logits == mx, lane, LANES), axis=-1, keepdims=True)
        vals.append(mx)
        idxs.append(am)
        logits = jnp.where(lane == am, -jnp.inf, logits)
    es = [jnp.exp(v - vals[0]) for v in vals]
    inv = 1.0 / sum(es)
    idx_out = jnp.zeros(logits.shape, jnp.int32)
    tw_out = jnp.zeros(logits.shape, F32)
    for k in range(TOP_K):
        idx_out = jnp.where(lane == k, idxs[k], idx_out)
        tw_out = jnp.where(lane == k, es[k] * inv, tw_out)
    idx_ref[...] = idx_out
    tw_ref[...] = tw_out


def _out_projection(ya, yb, yc, yd, w_out, x, gate1, n2_g, scale2, shift2, router_w, router_b):
    s, d = x.shape
    tm = _OUT_TM
    rw = jnp.zeros((d, LANES), F32).at[:, :N_EXPERTS].set(router_w)
    rb = jnp.full((1, LANES), _ROUTER_PAD_BIAS, F32).at[:, :N_EXPERTS].set(router_b)
    row = lambda i: (i, 0)
    vec = _const_spec((1, d))
    ysp = pl.BlockSpec((tm, GROUP_W), row)
    return pl.pallas_call(
        _outproj_kernel,
        grid=(s // tm,),
        in_specs=[ysp, ysp, ysp, ysp, _const_spec((d, d)), pl.BlockSpec((tm, d), row), vec, vec, vec, vec,
                  _const_spec((d, LANES)), _const_spec((1, LANES))],
        out_specs=[pl.BlockSpec((tm, d), row), pl.BlockSpec((tm, d), row),
                   pl.BlockSpec((tm, LANES), row), pl.BlockSpec((tm, LANES), row)],
        out_shape=[jax.ShapeDtypeStruct((s, d), F32), jax.ShapeDtypeStruct((s, d), F32),
                   jax.ShapeDtypeStruct((s, LANES), jnp.int32), jax.ShapeDtypeStruct((s, LANES), F32)],
        compiler_params=_cparams(("arbitrary",), 40),
        name="out_projection_router",
    )(ya, yb, yc, yd, w_out, x, gate1, n2_g, scale2, shift2, rw, rb)


_MOE_RT = 256
_MOE_CAP = 1536
_MOE_TJ = 256
_CMB_TT = 64


def _routing_tables(top_idx):
    s = top_idx.shape[0]
    n_assign = s * TOP_K
    flat_e = top_idx[:, :TOP_K].reshape(-1)
    onehot = (flat_e[:, None] == jnp.arange(N_EXPERTS, dtype=jnp.int32)[None, :]).astype(jnp.int32)
    csum = jnp.cumsum(onehot, axis=0)
    rank = jnp.sum(csum * onehot, axis=1) - 1
    counts = csum[-1]
    padded = (counts + _MOE_RT - 1) // _MOE_RT * _MOE_RT
    pstart = jnp.cumsum(padded) - padded
    dest = (pstart[flat_e] + rank).astype(jnp.int32)
    n_rows = n_assign + N_EXPERTS * _MOE_RT
    tok_rows = n_rows + _MOE_CAP
    row_tok = jnp.zeros((tok_rows,), jnp.int32).at[dest].set(jnp.arange(n_assign, dtype=jnp.int32) // TOP_K)
    n_items = N_EXPERTS + n_rows // _MOE_CAP
    per_e = (padded + _MOE_CAP - 1) // _MOE_CAP
    ends = jnp.cumsum(per_e)
    total = ends[-1]
    it = jnp.arange(n_items, dtype=jnp.int32)
    it_c = jnp.minimum(it, total - 1)
    e_i = jnp.searchsorted(ends, it_c, side="right").astype(jnp.int32)
    k_i = it_c - (ends - per_e)[e_i]
    live = it < total
    fill_start = jnp.sum(padded) + (it - total) * _MOE_CAP
    fill_rows = jnp.clip(n_rows - fill_start, 0, _MOE_CAP)
    start = jnp.where(live, pstart[e_i] + k_i * _MOE_CAP, jnp.minimum(fill_start, n_rows)).astype(jnp.int32)
    rows = jnp.where(live, jnp.clip(padded[e_i] - k_i * _MOE_CAP, 0, _MOE_CAP), -fill_rows).astype(jnp.int32)
    return dest, row_tok.reshape(tok_rows // LANES, 1, LANES), e_i, start, rows, n_rows, n_items


def _row_copy(src_hbm, src_row, dst, dst_row, sem):
    return pltpu.make_async_copy(src_hbm.at[pl.ds(src_row, 1)], dst.at[pl.ds(dst_row, 1)], sem)


def _moe_kernel(ie_ref, is_ref, ir_ref, tok_hbm, h_hbm, w1g_ref, w1l_ref, b1g_ref, b1l_ref, w2_ref, b2_ref,
                y_hbm, idx_smem, acc, xb, wg_bf, wl_bf, w2_bf, sem_idx, sem_g, sem_o):
    i = pl.program_id(0)
    j = pl.program_id(1)
    rt = _MOE_RT
    rows = ir_ref[i]
    start = is_ref[i]
    n_sub = rows // rt

    @pl.when(jnp.logical_and(j == 0, rows > 0))
    def _gather():
        cp = pltpu.make_async_copy(tok_hbm.at[pl.ds(start // LANES, _MOE_CAP // LANES)], idx_smem, sem_idx)
        cp.start()
        cp.wait()

        def issue(r, c):
            _row_copy(h_hbm, idx_smem[r // LANES, 0, r % LANES], acc, r, sem_g).start()
            return c

        lax.fori_loop(0, rows, issue, 0)

        def drain(r, c):
            _row_copy(h_hbm, 0, acc, r, sem_g).wait()
            return c

        lax.fori_loop(0, rows, drain, 0)

        def stage(t, c):
            r = pl.ds(pl.multiple_of(t * rt, rt), rt)
            xb[r, :] = acc[r, :].astype(BF16)
            acc[r, :] = jnp.zeros((rt, acc.shape[1]), F32)
            return c

        lax.fori_loop(0, n_sub, stage, 0)

    @pl.when(rows > 0)
    def _compute():
        wg_bf[...] = w1g_ref[0].astype(BF16)
        wl_bf[...] = w1l_ref[0].astype(BF16)
        w2_bf[...] = w2_ref[0].astype(BF16)

        def sub(t, c):
            r = pl.ds(pl.multiple_of(t * rt, rt), rt)
            x = xb[r, :]
            gate = jnp.dot(x, wg_bf[...], preferred_element_type=F32) + b1g_ref[0]
            lin = jnp.dot(x, wl_bf[...], preferred_element_type=F32) + b1l_ref[0]
            gate = jnp.minimum(gate, SWIGLU_LIMIT)
            lin = jnp.clip(lin, -SWIGLU_LIMIT, SWIGLU_LIMIT)
            act = gate * _sigmoid(SWIGLU_ALPHA * gate) * (lin + 1.0)
            acc[r, :] = acc[r, :] + jnp.dot(act.astype(BF16), w2_bf[...], preferred_element_type=F32)
            return c

        lax.fori_loop(0, n_sub, sub, 0)

    def out_copy(src_tile, dst_tile):
        r0 = pl.multiple_of(src_tile * rt, rt)
        d0 = pl.multiple_of(start + dst_tile * rt, rt)
        return pltpu.make_async_copy(acc.at[pl.ds(r0, rt)], y_hbm.at[pl.ds(d0, rt)], sem_o)

    @pl.when(jnp.logical_and(j == pl.num_programs(1) - 1, rows > 0))
    def _writeback():
        def put(t, c):
            r = pl.ds(pl.multiple_of(t * rt, rt), rt)
            acc[r, :] = acc[r, :] + b2_ref[0]
            out_copy(t, t).start()
            return c

        lax.fori_loop(0, n_sub, put, 0)

        def done(t, c):
            out_copy(t, t).wait()
            return c

        lax.fori_loop(0, n_sub, done, 0)

    @pl.when(jnp.logical_and(j == 0, rows < 0))
    def _clear_tail():
        n_fill = (-rows) // rt
        acc[0:rt, :] = jnp.zeros((rt, acc.shape[1]), F32)

        def put(t, c):
            out_copy(0, t).start()
            return c

        lax.fori_loop(0, n_fill, put, 0)

        def done(t, c):
            out_copy(0, t).wait()
            return c

        lax.fori_loop(0, n_fill, done, 0)


def _moe_ffn(h2, row_tok, item_e, item_start, item_rows, n_rows, n_items, w1, b1, w2, b2):
    d = h2.shape[1]
    n_j = D_EXPERT // _MOE_TJ
    tj = _MOE_TJ

    def jj(i, j, ir):
        return jnp.where(ir[i] > 0, j, n_j - 1)

    grid_spec = pltpu.PrefetchScalarGridSpec(
        num_scalar_prefetch=3,
        grid=(n_items, n_j),
        in_specs=[
            pl.BlockSpec(memory_space=pl.ANY),
            pl.BlockSpec(memory_space=pl.ANY),
            pl.BlockSpec((1, d, tj), lambda i, j, ie, ist, ir: (ie[i], 0, jj(i, j, ir))),
            pl.BlockSpec((1, d, tj), lambda i, j, ie, ist, ir: (ie[i], 0, n_j + jj(i, j, ir))),
            pl.BlockSpec((1, 1, tj), lambda i, j, ie, ist, ir: (ie[i], 0, jj(i, j, ir))),
            pl.BlockSpec((1, 1, tj), lambda i, j, ie, ist, ir: (ie[i], 0, n_j + jj(i, j, ir))),
            pl.BlockSpec((1, tj, d), lambda i, j, ie, ist, ir: (ie[i], jj(i, j, ir), 0)),
            pl.BlockSpec((1, 1, d), lambda i, j, ie, ist, ir: (ie[i], 0, 0)),
        ],
        out_specs=pl.BlockSpec(memory_space=pl.ANY),
        scratch_shapes=[
            pltpu.SMEM((_MOE_CAP // LANES, 1, LANES), jnp.int32),
            pltpu.VMEM((_MOE_CAP, d), F32),
            pltpu.VMEM((_MOE_CAP, d), BF16),
            pltpu.VMEM((d, tj), BF16), pltpu.VMEM((d, tj), BF16), pltpu.VMEM((tj, d), BF16),
            pltpu.SemaphoreType.DMA(()), pltpu.SemaphoreType.DMA(()), pltpu.SemaphoreType.DMA(()),
        ],
    )
    b1r = b1.reshape(N_EXPERTS, 1, 2 * D_EXPERT)
    return pl.pallas_call(
        _moe_kernel,
        grid_spec=grid_spec,
        out_shape=jax.ShapeDtypeStruct((n_rows, d), F32),
        compiler_params=_cparams(("arbitrary", "arbitrary"), 52),
        name="expert_ffn",
    )(item_e, item_start, item_rows, row_tok, h2, w1, w1, b1r, b1r, w2, b2.reshape(N_EXPERTS, 1, d))


def _combine_kernel(pos_hbm, y_hbm, x1_ref, tw_ref, gate_ref, fg_ref, o_ref, idx_smem, gbuf, sem_idx, sem_g,
                    *, final_norm):
    i = pl.program_id(0)
    tt = _CMB_TT
    n_idx = tt * TOP_K
    cp = pltpu.make_async_copy(pos_hbm.at[pl.ds(i * (n_idx // LANES), n_idx // LANES)], idx_smem, sem_idx)
    cp.start()
    cp.wait()

    def issue(q, c):
        _row_copy(y_hbm, idx_smem[q // LANES, 0, q % LANES], gbuf.at[q % TOP_K], q // TOP_K, sem_g).start()
        return c

    lax.fori_loop(0, n_idx, issue, 0)

    def drain(q, c):
        _row_copy(y_hbm, 0, gbuf.at[q % TOP_K], q // TOP_K, sem_g).wait()
        return c

    lax.fori_loop(0, n_idx, drain, 0)
    tw = tw_ref[...]
    mix = tw[:, 0:1] * gbuf[0]
    for k in range(1, TOP_K):
        mix = mix + tw[:, k:k + 1] * gbuf[k]
    out = x1_ref[...] + gate_ref[...] * mix
    if final_norm:
        out = _rms(out, fg_ref[...])
    o_ref[...] = out


def _combine(pos, y_sorted, x1, top_w, gate2, final_g):
    s, d = x1.shape
    tt = _CMB_TT
    final_norm = final_g is not None
    fg = final_g if final_norm else jnp.ones((1, d), F32)
    row = lambda i: (i, 0)
    return pl.pallas_call(
        functools.partial(_combine_kernel, final_norm=final_norm),
        grid=(s // tt,),
        in_specs=[
            pl.BlockSpec(memory_space=pl.ANY), pl.BlockSpec(memory_space=pl.ANY),
            pl.BlockSpec((tt, d), row), pl.BlockSpec((tt, LANES), row), _const_spec((1, d)), _const_spec((1, d)),
        ],
        out_specs=pl.BlockSpec((tt, d), row),
        out_shape=jax.ShapeDtypeStruct((s, d), F32),
        scratch_shapes=[
            pltpu.SMEM((tt * TOP_K // LANES, 1, LANES), jnp.int32),
            pltpu.VMEM((TOP_K, tt, d), F32),
            pltpu.SemaphoreType.DMA(()), pltpu.SemaphoreType.DMA(()),
        ],
        compiler_params=_cparams(("arbitrary",)),
        name="expert_combine",
    )(pos.reshape(s * TOP_K // LANES, 1, LANES), y_sorted, x1, top_w, gate2, fg)


def _moe_layer(h2, top_idx, top_w, x1, gate2, w1, b1, w2, b2, final_g):
    dest, row_tok, item_e, item_start, item_rows, n_rows, n_items = _routing_tables(top_idx)
    y_sorted = _moe_ffn(h2, row_tok, item_e, item_start, item_rows, n_rows, n_items, w1, b1, w2, b2)
    return _combine(dest, y_sorted, x1, top_w, gate2, final_g)


def kernel(x, c, ada_w, ada_b, norm1_g, w_in, sgu_ln_g, sgu_ln_b, sgu_w, sgu_b, conv_dw_w, conv_dw_b, conv_ln_g,
           conv_ln_b, conv_pw_w, conv_pw_b, gla_w2, gla_b2, gla_norm_g, merge_g, w_out, norm2_g, router_w, router_b,
           moe_w1, moe_b1, moe_w2, moe_b2, final_g):
    bsz, s, d = x.shape
    assert bsz == 1 and d == D_MODEL
    depth = ada_w.shape[0]
    xs = x.reshape(s, d)
    mod = _modulation(c, ada_w, ada_b)
    row = lambda v: v.reshape(1, -1)
    for l in range(depth):
        shift1, scale1, gate1, shift2, scale2, gate2 = [mod[l, :, k * d:(k + 1) * d] for k in range(6)]
        mg = merge_g[l]
        z_a, z_b, z_c, z_d = _in_projection(xs, row(norm1_g[l]), scale1, shift1, _relayout_w_in(w_in[l]))
        y_a = _spatial_gating(z_a, row(sgu_ln_g[l]), row(sgu_ln_b[l]), sgu_w[l], sgu_b[l], row(mg[:GROUP_W]))
        y_b = _conv_module(z_b, conv_dw_w[l], row(conv_dw_b[l]), row(conv_ln_g[l]), row(conv_ln_b[l]),
                           conv_pw_w[l], row(conv_pw_b[l]), row(mg[GROUP_W:2 * GROUP_W]))
        y_c = _gla(z_c, gla_w2[l], row(gla_b2[l]), row(gla_norm_g[l]), row(mg[2 * GROUP_W:3 * GROUP_W]))
        y_d = _stick_breaking(z_d, row(mg[3 * GROUP_W:]))
        x1, h2, top_idx, top_w = _out_projection(y_a, y_b, y_c, y_d, w_out[l].astype(BF16), xs, gate1,
                                                 row(norm2_g[l]), scale2, shift2, router_w[l], row(router_b[l]))
        xs = _moe_layer(h2, top_idx, top_w, x1, gate2, moe_w1[l], moe_b1[l], moe_w2[l], moe_b2[l],
                        row(final_g) if l == depth - 1 else None)
    return xs.reshape(bsz, s, d)
```

```python
import functools

import numpy as np
import jax
import jax.numpy as jnp
from jax import lax
from jax.experimental import pallas as pl
from jax.experimental.pallas import tpu as pltpu

F32 = jnp.float32
BF16 = jnp.bfloat16

D_MODEL = 2048
GROUP_W = 512
N_GROUPS = 4
EPS = 1e-6
CHUNK = 64
SGU_BLOCK = 128
SGU_HEADS = 4
CONV_WIDTH = 31
GLA_HEADS = 4
GLA_DK = 64
GLA_DV = 128
GLA_RANK = 16
GLA_GATE_NORM = 16.0
C_QK = GLA_HEADS * GLA_DK
SB_HEADS = 4
SB_HEAD_DIM = 128
N_EXPERTS = 32
TOP_K = 4
D_EXPERT = 2048
SWIGLU_LIMIT = 7.0
SWIGLU_ALPHA = 1.702

LANES = 128
SUBLANES = 8
VMEM_BYTES_V7X = 64 * 1024 * 1024

ZA_W = 2 * GROUP_W
ZB_W = 2 * GROUP_W
ZC_LR_OFF = 2 * C_QK + 2 * GROUP_W
ZC_W = ZC_LR_OFF + LANES
ZD_W = 3 * GROUP_W


def _cparams(sem, vmem_mb=None):
    kw = dict(dimension_semantics=sem)
    if vmem_mb is not None:
        kw["vmem_limit_bytes"] = vmem_mb * 1024 * 1024
    return pltpu.CompilerParams(**kw)


def _const_spec(shape):
    nd = len(shape)
    return pl.BlockSpec(shape, lambda *_: (0,) * nd, pipeline_mode=pl.Buffered(1))


def _rms(x, g):
    return x * lax.rsqrt(jnp.mean(x * x, axis=-1, keepdims=True) + EPS) * g


def _layer_norm(x, g, b):
    mu = jnp.mean(x, axis=-1, keepdims=True)
    xc = x - mu
    return xc * lax.rsqrt(jnp.mean(xc * xc, axis=-1, keepdims=True) + EPS) * g + b


def _sigmoid(x):
    return 1.0 / (1.0 + jnp.exp(-x))


def _silu(x):
    return x * _sigmoid(x)


_MOD_TN = 1024
_MOD_ROWS = 64


def _mod_kernel(c_ref, w_ref, b_ref, o_ref):
    d = w_ref.shape[1]
    tn = w_ref.shape[2]

    def body(i, acc):
        r0 = pl.multiple_of(i * _MOD_ROWS, _MOD_ROWS)
        c = c_ref[pl.ds(r0, _MOD_ROWS), :]
        p = w_ref[0, pl.ds(r0, _MOD_ROWS), :] * _silu(c)
        return acc + p.reshape(_MOD_ROWS // SUBLANES, SUBLANES, tn).sum(axis=0)

    acc = lax.fori_loop(0, d // _MOD_ROWS, body, jnp.zeros((SUBLANES, tn), F32))
    o_ref[0] = jnp.sum(acc, axis=0, keepdims=True) + b_ref[0]


def _modulation(c, ada_w, ada_b):
    n_l, d, n = ada_w.shape
    return pl.pallas_call(
        _mod_kernel,
        grid=(n_l, n // _MOD_TN),
        in_specs=[
            pl.BlockSpec((d, 1), lambda l, j: (0, 0)),
            pl.BlockSpec((1, d, _MOD_TN), lambda l, j: (l, 0, j)),
            pl.BlockSpec((1, 1, _MOD_TN), lambda l, j: (l, 0, j)),
        ],
        out_specs=pl.BlockSpec((1, 1, _MOD_TN), lambda l, j: (l, 0, j)),
        out_shape=jax.ShapeDtypeStruct((n_l, 1, n), F32),
        compiler_params=_cparams(("arbitrary", "arbitrary"), 40),
        name="adaln_modulation",
    )(c.reshape(d, 1), ada_w, ada_b.reshape(n_l, 1, n))


_INPROJ_TM = 256
_INPROJ_CH = 512


def _inproj_kernel(x_ref, g_ref, sc_ref, sh_ref, w_ref, za_ref, zb_ref, zc_ref, zd_ref, h_scr):
    x = x_ref[...]
    h = _rms(x, g_ref[...]) * (1.0 + sc_ref[...]) + sh_ref[...]
    h_scr[...] = h.astype(BF16)
    off = 0
    for ref in (za_ref, zb_ref, zc_ref, zd_ref):
        n = ref.shape[1]
        for c0 in range(0, n, _INPROJ_CH):
            c1 = min(c0 + _INPROJ_CH, n)
            ref[:, c0:c1] = jnp.dot(h_scr[...], w_ref[:, off + c0:off + c1],
                                    preferred_element_type=F32).astype(BF16)
        off += n


def _in_projection(x, g, scale, shift, w_p):
    s, d = x.shape
    tm = _INPROJ_TM
    widths = (ZA_W, ZB_W, ZC_W, ZD_W)
    row = lambda i: (i, 0)
    return pl.pallas_call(
        _inproj_kernel,
        grid=(s // tm,),
        in_specs=[
            pl.BlockSpec((tm, d), row),
            _const_spec((1, d)), _const_spec((1, d)), _const_spec((1, d)),
            _const_spec(w_p.shape),
        ],
        out_specs=[pl.BlockSpec((tm, w), row) for w in widths],
        out_shape=[jax.ShapeDtypeStruct((s, w), BF16) for w in widths],
        scratch_shapes=[pltpu.VMEM((tm, d), BF16)],
        compiler_params=_cparams(("arbitrary",), 48),
        name="norm_in_projection",
    )(x, g, scale, shift, w_p)


_SGU_TA = 256
_GELU_C = float(np.sqrt(2.0 / np.pi))


def _gelu_tanh(x):
    return 0.5 * x * (1.0 + jnp.tanh(_GELU_C * (x + 0.044715 * (x * x * x))))


def _sgu_kernel(z_ref, lng_ref, lnb_ref, w_ref, bs_ref, mg_ref, o_ref, y_scr):
    ta = z_ref.shape[0]
    hd = GROUP_W // SGU_HEADS
    g = _gelu_tanh(z_ref[...].astype(F32))
    u = g[:, :GROUP_W]
    v = _layer_norm(g[:, GROUP_W:], lng_ref[...], lnb_ref[...]).astype(BF16)
    t_pos = lax.broadcasted_iota(jnp.int32, (SGU_BLOCK, SGU_BLOCK), 0)
    s_pos = lax.broadcasted_iota(jnp.int32, (SGU_BLOCK, SGU_BLOCK), 1)
    mask = (s_pos // CHUNK) <= (t_pos // CHUNK)
    for h in range(SGU_HEADS):
        wm = jnp.where(mask, w_ref[h], 0.0).astype(BF16)
        for blk in range(ta // SGU_BLOCK):
            r0 = blk * SGU_BLOCK
            vb = v[r0:r0 + SGU_BLOCK, h * hd:(h + 1) * hd]
            s_out = jnp.dot(wm, vb, preferred_element_type=F32) + bs_ref[h]
            y_scr[r0:r0 + SGU_BLOCK, h * hd:(h + 1) * hd] = u[r0:r0 + SGU_BLOCK, h * hd:(h + 1) * hd] * s_out
    o_ref[...] = _rms(y_scr[...], mg_ref[...]).astype(BF16)


def _spatial_gating(z_a, ln_g, ln_b, w_s, b_s, merge_g):
    s = z_a.shape[0]
    ta = _SGU_TA
    return pl.pallas_call(
        _sgu_kernel,
        grid=(s // ta,),
        in_specs=[
            pl.BlockSpec((ta, ZA_W), lambda i: (i, 0)),
            _const_spec((1, GROUP_W)), _const_spec((1, GROUP_W)),
            _const_spec(w_s.shape), _const_spec((SGU_HEADS, SGU_BLOCK, 1)),
            _const_spec((1, GROUP_W)),
        ],
        out_specs=pl.BlockSpec((ta, GROUP_W), lambda i: (i, 0)),
        out_shape=jax.ShapeDtypeStruct((s, GROUP_W), BF16),
        scratch_shapes=[pltpu.VMEM((ta, GROUP_W), F32)],
        compiler_params=_cparams(("arbitrary",)),
        name="mixer_spatial_gating",
    )(z_a, ln_g, ln_b, w_s, b_s.reshape(SGU_HEADS, SGU_BLOCK, 1), merge_g)


_CONV_TB = 512
_CONV_HALO = 32
_CONV_RCH = 64


def _glu(z):
    return z[:, :GROUP_W] * _sigmoid(z[:, GROUP_W:])


def _conv_kernel(zp_ref, z_ref, dww_ref, dwb_ref, lng_ref, lnb_ref, pww_ref, pwb_ref, mg_ref, o_ref,
                 h_scr, c_scr):
    tb = z_ref.shape[0]
    first = pl.program_id(0) == 0
    halo = _glu(zp_ref[...].astype(F32))
    h_scr[0:_CONV_HALO, :] = jnp.where(first, 0.0, halo)
    h_scr[_CONV_HALO:, :] = _glu(z_ref[...].astype(F32))
    lead = _CONV_HALO - (CONV_WIDTH - 1)
    for r0 in range(0, tb, _CONV_RCH):
        acc = jnp.zeros((_CONV_RCH, GROUP_W), F32)
        for k in range(CONV_WIDTH):
            acc = acc + dww_ref[k:k + 1, :] * h_scr[r0 + lead + k:r0 + lead + k + _CONV_RCH, :]
        c_scr[r0:r0 + _CONV_RCH, :] = acc
    hc = _silu(_layer_norm(c_scr[...] + dwb_ref[...], lng_ref[...], lnb_ref[...]))
    y = jnp.dot(hc.astype(BF16), pww_ref[...], preferred_element_type=F32) + pwb_ref[...]
    o_ref[...] = _rms(y, mg_ref[...]).astype(BF16)


def _conv_module(z_b, dw_w, dw_b, ln_g, ln_b, pw_w, pw_b, merge_g):
    s = z_b.shape[0]
    tb = _CONV_TB
    per = tb // _CONV_HALO
    vec = _const_spec((1, GROUP_W))
    return pl.pallas_call(
        _conv_kernel,
        grid=(s // tb,),
        in_specs=[
            pl.BlockSpec((_CONV_HALO, ZB_W), lambda i: (jnp.maximum(i * per - 1, 0), 0)),
            pl.BlockSpec((tb, ZB_W), lambda i: (i, 0)),
            _const_spec((CONV_WIDTH, GROUP_W)), vec, vec, vec,
            _const_spec((GROUP_W, GROUP_W)), vec, vec,
        ],
        out_specs=pl.BlockSpec((tb, GROUP_W), lambda i: (i, 0)),
        out_shape=jax.ShapeDtypeStruct((s, GROUP_W), BF16),
        scratch_shapes=[pltpu.VMEM((tb + _CONV_HALO, GROUP_W), F32), pltpu.VMEM((tb, GROUP_W), F32)],
        compiler_params=_cparams(("arbitrary",)),
        name="mixer_conv_module",
    )(z_b, z_b, dw_w, dw_b, ln_g, ln_b, pw_w.astype(BF16), pw_b, merge_g)


def _relayout_w_in(w_in):
    d = w_in.shape[0]
    off_c = ZA_W + ZB_W
    c_qkv = 2 * C_QK + GROUP_W
    off_d = off_c + c_qkv + GLA_RANK + GROUP_W
    return jnp.concatenate([
        w_in[:, :off_c + c_qkv],
        w_in[:, off_c + c_qkv + GLA_RANK:off_d],
        w_in[:, off_c + c_qkv:off_c + c_qkv + GLA_RANK],
        jnp.zeros((d, LANES - GLA_RANK), w_in.dtype),
        w_in[:, off_d:],
    ], axis=1).astype(BF16)


_GLA_TC = 512
_GLA_LEVELS = 6


def _gla_constants():
    c = CHUNK
    t = np.arange(c)
    heads = np.arange(GLA_HEADS)
    mats = [(t[None, :] <= t[:, None])]
    q_masks, k_masks, a_masks = [], [], []
    head_eq = np.repeat(np.repeat(np.eye(GLA_HEADS, dtype=bool), c, axis=0), GLA_DK, axis=1)
    m = c // 2
    while m >= 1:
        r = (t // (2 * m)) * 2 * m + m - 1
        lo, hi = np.minimum(t, r), np.maximum(t, r)
        mats.append((t[None, :] > lo[:, None]) & (t[None, :] <= hi[:, None]))
        upper = (t % (2 * m)) >= m
        q_masks.append(head_eq & np.tile(upper, GLA_HEADS)[:, None])
        k_masks.append(np.broadcast_to((~upper)[:, None], (c, C_QK)))
        a_masks.append(np.tile((t[:, None] // (2 * m)) == (t[None, :] // (2 * m)), (GLA_HEADS, 1)))
        m //= 2
    q_masks.append(head_eq)
    k_masks.append(np.ones((c, C_QK), bool))
    a_masks.append(np.tile(np.eye(c, dtype=bool), (GLA_HEADS, 1)))
    cmat = np.concatenate(mats, axis=0).astype(np.float32)
    cmat3 = np.concatenate([cmat] * 3, axis=1)
    return (jnp.asarray(cmat3, BF16), jnp.asarray(np.stack(q_masks), F32),
            jnp.asarray(np.stack(k_masks), F32), jnp.asarray(np.stack(a_masks), F32))


def _split3(x):
    hi = x.astype(BF16)
    r1 = x - hi.astype(F32)
    mid = r1.astype(BF16)
    lo = (r1 - mid.astype(F32)).astype(BF16)
    return hi, mid, lo


def _log_sigmoid(x):
    return jnp.minimum(x, 0.0) - jnp.log1p(jnp.exp(-jnp.abs(x)))


_NT = (((1,), (1,)), ((), ()))
_TN = (((0,), (0,)), ((), ()))


def _gla_kernel(z_ref, w2_ref, b2_ref, ng_ref, mg_ref, cm_ref, qm_ref, km_ref, am_ref, o_ref, s_scr):
    c = CHUNK
    n_lv = _GLA_LEVELS

    @pl.when(pl.program_id(0) == 0)
    def _():
        s_scr[...] = jnp.zeros_like(s_scr)

    ones_col = jnp.ones((3 * c, LANES), BF16)

    def chunk(ci, carry):
        r0 = pl.multiple_of(ci * c, c)
        rows = pl.ds(r0, c)
        qs = z_ref[rows, 0:C_QK].astype(F32) * (GLA_DK ** -0.5)
        k = z_ref[rows, C_QK:2 * C_QK].astype(F32)
        v = z_ref[rows, 2 * C_QK:2 * C_QK + GROUP_W]
        gout = z_ref[rows, 2 * C_QK + GROUP_W:ZC_LR_OFF].astype(F32)
        lr = z_ref[rows, ZC_LR_OFF:ZC_W]
        pre = jnp.dot(jnp.concatenate([lr, lr, lr], axis=1), w2_ref[...], preferred_element_type=F32) + b2_ref[...]
        gk = _log_sigmoid(pre) * (1.0 / GLA_GATE_NORM)
        gs = jnp.concatenate(_split3(gk), axis=0)
        ball = jnp.dot(cm_ref[...], gs, preferred_element_type=F32)
        b = ball[0:c]
        a_acc = jnp.zeros((GLA_HEADS * c, c), F32)
        for lv in range(n_lv + 1):
            if lv < n_lv:
                e = jnp.exp(ball[(lv + 1) * c:(lv + 2) * c])
                qe, ke = qs * e, k * e
            else:
                qe, ke = qs, k
            qst = (jnp.concatenate([qe] * GLA_HEADS, axis=0) * qm_ref[lv]).astype(BF16)
            kk = (ke * km_ref[lv]).astype(BF16)
            a = lax.dot_general(qst, kk, _NT, preferred_element_type=F32)
            a_acc = a_acc + a * am_ref[lv]
        qb = (jnp.concatenate([qs * jnp.exp(b)] * GLA_HEADS, axis=0) * qm_ref[n_lv]).astype(BF16)
        o_all = (jnp.dot(a_acc.astype(BF16), v, preferred_element_type=F32)
                 + jnp.dot(qb, s_scr[...].astype(BF16), preferred_element_type=F32))
        kd = (k * jnp.exp(b[c - 1:c, :] - b)).astype(BF16)
        upd = lax.dot_general(kd, v, _TN, preferred_element_type=F32)
        dcol = jnp.exp(lax.dot_general(gs, ones_col, _TN, preferred_element_type=F32))
        s_scr[...] = s_scr[...] * jnp.concatenate([dcol] * GLA_HEADS, axis=1) + upd
        outs = []
        for h in range(GLA_HEADS):
            oh = o_all[h * c:(h + 1) * c, h * GLA_DV:(h + 1) * GLA_DV]
            outs.append(_rms(oh, ng_ref[...]) * _silu(gout[:, h * GLA_DV:(h + 1) * GLA_DV]))
        y = jnp.concatenate(outs, axis=1)
        o_ref[rows, :] = _rms(y, mg_ref[...]).astype(BF16)
        return carry

    lax.fori_loop(0, z_ref.shape[0] // c, chunk, 0)


def _gla(z_c, w2, b2, norm_g, merge_g):
    s = z_c.shape[0]
    tc = _GLA_TC
    cm, qm, km, am = _gla_constants()
    w2p = jnp.zeros((LANES, C_QK), F32).at[:GLA_RANK].set(w2)
    w2s = jnp.concatenate(_split3(w2p), axis=0)
    return pl.pallas_call(
        _gla_kernel,
        grid=(s // tc,),
        in_specs=[
            pl.BlockSpec((tc, ZC_W), lambda i: (i, 0)),
            _const_spec(w2s.shape), _const_spec((1, C_QK)), _const_spec((1, GLA_DV)), _const_spec((1, GROUP_W)),
            _const_spec(cm.shape), _const_spec(qm.shape), _const_spec(km.shape), _const_spec(am.shape),
        ],
        out_specs=pl.BlockSpec((tc, GROUP_W), lambda i: (i, 0)),
        out_shape=jax.ShapeDtypeStruct((s, GROUP_W), BF16),
        scratch_shapes=[pltpu.VMEM((GLA_HEADS * GLA_DK, GLA_HEADS * GLA_DV), F32)],
        compiler_params=_cparams(("arbitrary",)),
        name="mixer_gated_linear_attention",
    )(z_c, w2s, b2, norm_g, merge_g, cm, qm, km, am)


_SB_TQ = 256
_SB_TK = 128
_SB_STOP = -110.0


def _sb_kernel(q_ref, k_ref, v_ref, mg_ref, o_ref, acc_scr, car_scr):
    i = pl.program_id(0)
    tq, tk, hd = _SB_TQ, _SB_TK, SB_HEAD_DIM
    scale = hd ** -0.5
    acc_scr[...] = jnp.zeros_like(acc_scr)
    car_scr[...] = jnp.zeros_like(car_scr)
    jj = lax.broadcasted_iota(jnp.int32, (2 * tk, tk), 0) % tk
    ss = lax.broadcasted_iota(jnp.int32, (2 * tk, tk), 1)
    suffix = jnp.where(jj > ss, 1.0, 0.0).astype(BF16)
    q_pos = i * tq + lax.broadcasted_iota(jnp.int32, (tq, tk), 0)
    col = lax.broadcasted_iota(jnp.int32, (tq, tk), 1)

    def cond(st):
        j, cmax = st
        return jnp.logical_and(j >= 0, cmax > _SB_STOP)

    def body(st):
        j, _ = st
        k0 = pl.multiple_of(j * tk, tk)
        valid = (k0 + col) < q_pos
        cmax = jnp.float32(-jnp.inf)
        for h in range(SB_HEADS):
            hs = slice(h * hd, (h + 1) * hd)
            l = lax.dot_general(q_ref[:, hs], k_ref[pl.ds(k0, tk), hs], _NT, preferred_element_type=F32) * scale
            sp = jnp.log1p(jnp.exp(-jnp.abs(l)))
            ls_pos = jnp.minimum(l, 0.0) - sp
            log_1m = jnp.where(valid, ls_pos - l, 0.0)
            hi = log_1m.astype(BF16)
            lo = (log_1m - hi.astype(F32)).astype(BF16)
            rest_in = jnp.dot(jnp.concatenate([hi, lo], axis=1), suffix, preferred_element_type=F32)
            car = car_scr[h]
            w = jnp.where(valid, jnp.exp(ls_pos + rest_in + car), 0.0)
            acc_scr[h] = acc_scr[h] + jnp.dot(w.astype(BF16), v_ref[pl.ds(k0, tk), hs], preferred_element_type=F32)
            car = car + (rest_in[:, 0:1] + log_1m[:, 0:1])
            car_scr[h] = car
            cmax = jnp.maximum(cmax, jnp.max(car))
        return j - 1, cmax

    lax.while_loop(cond, body, ((i + 1) * (tq // tk) - 1, jnp.float32(0.0)))
    y = jnp.concatenate([acc_scr[h] for h in range(SB_HEADS)], axis=1)
    o_ref[...] = _rms(y, mg_ref[...]).astype(BF16)


def _stick_breaking(z_d, merge_g):
    s = z_d.shape[0]
    tq = _SB_TQ
    kv = lambda c: pl.BlockSpec((s, GROUP_W), lambda i: (0, c), pipeline_mode=pl.Buffered(1))
    return pl.pallas_call(
        _sb_kernel,
        grid=(s // tq,),
        in_specs=[pl.BlockSpec((tq, GROUP_W), lambda i: (i, 0)), kv(1), kv(2), _const_spec((1, GROUP_W))],
        out_specs=pl.BlockSpec((tq, GROUP_W), lambda i: (i, 0)),
        out_shape=jax.ShapeDtypeStruct((s, GROUP_W), BF16),
        scratch_shapes=[pltpu.VMEM((SB_HEADS, tq, SB_HEAD_DIM), F32), pltpu.VMEM((SB_HEADS, tq, _SB_TK), F32)],
        compiler_params=_cparams(("arbitrary",), 40),
        name="mixer_stick_breaking",
    )(z_d, z_d, z_d, merge_g)


_OUT_TM = 256
_ROUTER_PAD_BIAS = -1e30


def _outproj_kernel(ya_ref, yb_ref, yc_ref, yd_ref, w_ref, x_ref, g1_ref, n2_ref, sc_ref, sh_ref, rw_ref, rb_ref,
                    x1_ref, h2_ref, idx_ref, tw_ref):
    m = jnp.zeros(x_ref.shape, F32)
    for g, y_ref in enumerate((ya_ref, yb_ref, yc_ref, yd_ref)):
        m = m + jnp.dot(y_ref[...], w_ref[g * GROUP_W:(g + 1) * GROUP_W, :], preferred_element_type=F32)
    x1 = x_ref[...] + g1_ref[...] * m
    x1_ref[...] = x1
    h2 = _rms(x1, n2_ref[...]) * (1.0 + sc_ref[...]) + sh_ref[...]
    h2_ref[...] = h2
    h_hi = h2.astype(BF16)
    h_lo = (h2 - h_hi.astype(F32)).astype(BF16)
    logits = jnp.dot(jnp.concatenate([h_hi, h_hi, h_lo], axis=1), rw_ref[...], preferred_element_type=F32) + rb_ref[...]
    lane = lax.broadcasted_iota(jnp.int32, logits.shape, 1)
    vals, idxs = [], []
    for _ in range(TOP_K):
        mx = jnp.max(logits, axis=-1, keepdims=True)
        am = jnp.min(jnp.where(logits == mx, lane, LANES), axis=-1, keepdims=True)
        vals.append(mx)
        idxs.append(am)
        logits = jnp.where(lane == am, -jnp.inf, logits)
    es = [jnp.exp(v - vals[0]) for v in vals]
    inv = 1.0 / sum(es)
    idx_out = jnp.zeros(logits.shape, jnp.int32)
    tw_out = jnp.zeros(logits.shape, F32)
    for k in range(TOP_K):
        idx_out = jnp.where(lane == k, idxs[k], idx_out)
        tw_out = jnp.where(lane == k, es[k] * inv, tw_out)
    idx_ref[...] = idx_out
    tw_ref[...] = tw_out


def _out_projection(ya, yb, yc, yd, w_out, x, gate1, n2_g, scale2, shift2, router_w, router_b):
    s, d = x.shape
    tm = _OUT_TM
    rw = jnp.zeros((d, LANES), F32).at[:, :N_EXPERTS].set(router_w)
    rw_hi = rw.astype(BF16)
    rw_lo = (rw - rw_hi.astype(F32)).astype(BF16)
    rw = jnp.concatenate([rw_hi, rw_lo, rw_hi], axis=0)
    rb = jnp.full((1, LANES), _ROUTER_PAD_BIAS, F32).at[:, :N_EXPERTS].set(router_b)
    row = lambda i: (i, 0)
    vec = _const_spec((1, d))
    ysp = pl.BlockSpec((tm, GROUP_W), row)
    return pl.pallas_call(
        _outproj_kernel,
        grid=(s // tm,),
        in_specs=[ysp, ysp, ysp, ysp, _const_spec((d, d)), pl.BlockSpec((tm, d), row), vec, vec, vec, vec,
                  _const_spec((3 * d, LANES)), _const_spec((1, LANES))],
        out_specs=[pl.BlockSpec((tm, d), row), pl.BlockSpec((tm, d), row),
                   pl.BlockSpec((tm, LANES), row), pl.BlockSpec((tm, LANES), row)],
        out_shape=[jax.ShapeDtypeStruct((s, d), F32), jax.ShapeDtypeStruct((s, d), F32),
                   jax.ShapeDtypeStruct((s, LANES), jnp.int32), jax.ShapeDtypeStruct((s, LANES), F32)],
        compiler_params=_cparams(("arbitrary",), 40),
        name="out_projection_router",
    )(ya, yb, yc, yd, w_out, x, gate1, n2_g, scale2, shift2, rw, rb)


_MOE_RT = 256
_MOE_CAP = 1536
_MOE_TJ = 256
_CMB_TT = 128


def _routing_tables(top_idx):
    s = top_idx.shape[0]
    n_assign = s * TOP_K
    flat_e = top_idx[:, :TOP_K].reshape(-1)
    onehot = (flat_e[:, None] == jnp.arange(N_EXPERTS, dtype=jnp.int32)[None, :]).astype(jnp.int32)
    csum = jnp.cumsum(onehot, axis=0)
    rank = jnp.sum(csum * onehot, axis=1) - 1
    counts = csum[-1]
    padded = (counts + _MOE_RT - 1) // _MOE_RT * _MOE_RT
    pstart = jnp.cumsum(padded) - padded
    dest = (pstart[flat_e] + rank).astype(jnp.int32)
    n_rows = n_assign + N_EXPERTS * _MOE_RT
    tok_rows = n_rows + _MOE_CAP
    row_tok = jnp.zeros((tok_rows,), jnp.int32).at[dest].set(jnp.arange(n_assign, dtype=jnp.int32) // TOP_K)
    n_items = N_EXPERTS + n_rows // _MOE_CAP
    per_e = (padded + _MOE_CAP - 1) // _MOE_CAP
    ends = jnp.cumsum(per_e)
    total = ends[-1]
    it = jnp.arange(n_items, dtype=jnp.int32)
    it_c = jnp.minimum(it, total - 1)
    e_i = jnp.sum((ends[None, :] <= it_c[:, None]).astype(jnp.int32), axis=1)
    k_i = it_c - (ends - per_e)[e_i]
    live = it < total
    fill_start = jnp.sum(padded) + (it - total) * _MOE_CAP
    fill_rows = jnp.clip(n_rows - fill_start, 0, _MOE_CAP)
    start = jnp.where(live, pstart[e_i] + k_i * _MOE_CAP, jnp.minimum(fill_start, n_rows)).astype(jnp.int32)
    rows = jnp.where(live, jnp.clip(padded[e_i] - k_i * _MOE_CAP, 0, _MOE_CAP), -fill_rows).astype(jnp.int32)
    return dest, row_tok.reshape(tok_rows // LANES, 1, LANES), e_i, start, rows, n_rows, n_items


def _row_copy(src_hbm, src_row, dst, dst_row, sem):
    return pltpu.make_async_copy(src_hbm.at[pl.ds(src_row, 1)], dst.at[pl.ds(dst_row, 1)], sem)


_ROW_UNROLL = 8


def _issue_rows(n_rows, idx_smem, copy_for):
    per_line = LANES // _ROW_UNROLL

    def group(g, c):
        line = lax.shift_right_logical(g, per_line.bit_length() - 1)
        lane0 = jnp.bitwise_and(g, per_line - 1) * _ROW_UNROLL
        for u in range(_ROW_UNROLL):
            copy_for(idx_smem[line, 0, lane0 + u], g, u).start()
        return c

    if isinstance(n_rows, int):
        n_groups = n_rows // _ROW_UNROLL
    else:
        n_groups = lax.shift_right_logical(n_rows, _ROW_UNROLL.bit_length() - 1)
    lax.fori_loop(0, n_groups, group, 0)


def _moe_kernel(ie_ref, is_ref, ir_ref, tok_hbm, h_hbm, w1g_ref, w1l_ref, b1g_ref, b1l_ref, w2_ref, b2_ref,
                y_hbm, idx_smem, land, acc, xb, wg_bf, wl_bf, w2_bf, sem_idx, sem_g, sem_o):
    i = pl.program_id(0)
    j = pl.program_id(1)
    rt = _MOE_RT
    rows = ir_ref[i]
    start = is_ref[i]
    n_sub = rows // rt

    def gather_start(item):
        first_line = lax.shift_right_logical(is_ref[item], LANES.bit_length() - 1)
        cp = pltpu.make_async_copy(tok_hbm.at[pl.ds(first_line, _MOE_CAP // LANES)], idx_smem, sem_idx)
        cp.start()
        cp.wait()
        _issue_rows(ir_ref[item], idx_smem,
                    lambda tok, g, u: _row_copy(h_hbm, tok, land, g * _ROW_UNROLL + u, sem_g))

    @pl.when(jnp.logical_and(jnp.logical_and(i == 0, j == 0), rows > 0))
    def _first_gather():
        gather_start(0)

    @pl.when(jnp.logical_and(j == 0, rows > 0))
    def _stage():
        def landed(t, c):
            pltpu.make_async_copy(h_hbm.at[pl.ds(0, rt)], land.at[pl.ds(pl.multiple_of(t * rt, rt), rt)], sem_g).wait()
            return c

        lax.fori_loop(0, n_sub, landed, 0)

        def cast(t, c):
            r = pl.ds(pl.multiple_of(t * rt, rt), rt)
            xb[r, :] = land[r, :].astype(BF16)
            return c

        lax.fori_loop(0, n_sub, cast, 0)

    nxt = jnp.minimum(i + 1, pl.num_programs(0) - 1)

    @pl.when(jnp.logical_and(jnp.logical_and(j == 1, i + 1 < pl.num_programs(0)), ir_ref[nxt] > 0))
    def _next_gather():
        gather_start(nxt)

    def ffn(t, accumulate):
        r = pl.ds(pl.multiple_of(t * rt, rt), rt)
        x = xb[r, :]
        gate = jnp.dot(x, wg_bf[...], preferred_element_type=F32) + b1g_ref[0, 0]
        lin = jnp.dot(x, wl_bf[...], preferred_element_type=F32) + b1l_ref[0, 0]
        gate = jnp.minimum(gate, SWIGLU_LIMIT)
        lin = jnp.clip(lin, -SWIGLU_LIMIT, SWIGLU_LIMIT)
        act = gate * _sigmoid(SWIGLU_ALPHA * gate) * (lin + 1.0)
        y = jnp.dot(act.astype(BF16), w2_bf[...], preferred_element_type=F32)
        acc[r, :] = acc[r, :] + y if accumulate else y

    @pl.when(rows > 0)
    def _compute():
        wg_bf[...] = w1g_ref[0, 0].astype(BF16)
        wl_bf[...] = w1l_ref[0, 0].astype(BF16)
        w2_bf[...] = w2_ref[0, 0].astype(BF16)

        @pl.when(j == 0)
        def _():
            lax.fori_loop(0, n_sub, lambda t, c: (ffn(t, False), c)[1], 0)

        @pl.when(j > 0)
        def _():
            lax.fori_loop(0, n_sub, lambda t, c: (ffn(t, True), c)[1], 0)

    def out_copy(src_tile, dst_tile):
        r0 = pl.multiple_of(src_tile * rt, rt)
        d0 = pl.multiple_of(start + dst_tile * rt, rt)
        return pltpu.make_async_copy(acc.at[pl.ds(r0, rt)], y_hbm.at[pl.ds(d0, rt)], sem_o)

    @pl.when(jnp.logical_and(j == pl.num_programs(1) - 1, rows > 0))
    def _writeback():
        def put(t, c):
            r = pl.ds(pl.multiple_of(t * rt, rt), rt)
            acc[r, :] = acc[r, :] + b2_ref[0, 0]
            out_copy(t, t).start()
            return c

        lax.fori_loop(0, n_sub, put, 0)

        def done(t, c):
            out_copy(t, t).wait()
            return c

        lax.fori_loop(0, n_sub, done, 0)

    @pl.when(jnp.logical_and(j == 0, rows < 0))
    def _clear_tail():
        n_fill = (-rows) // rt
        acc[0:rt, :] = jnp.zeros((rt, acc.shape[1]), F32)

        def put(t, c):
            out_copy(0, t).start()
            return c

        lax.fori_loop(0, n_fill, put, 0)

        def done(t, c):
            out_copy(0, t).wait()
            return c

        lax.fori_loop(0, n_fill, done, 0)


def _moe_ffn(h2, row_tok, item_e, item_start, item_rows, n_rows, n_items, w1, b1, w2, b2, layer):
    d = h2.shape[1]
    n_j = D_EXPERT // _MOE_TJ
    tj = _MOE_TJ
    depth = w1.shape[0]

    def jj(i, j, ir):
        return jnp.where(ir[i] > 0, j, n_j - 1)

    grid_spec = pltpu.PrefetchScalarGridSpec(
        num_scalar_prefetch=3,
        grid=(n_items, n_j),
        in_specs=[
            pl.BlockSpec(memory_space=pl.ANY),
            pl.BlockSpec(memory_space=pl.ANY),
            pl.BlockSpec((1, 1, d, tj), lambda i, j, ie, ist, ir: (layer, ie[i], 0, jj(i, j, ir))),
            pl.BlockSpec((1, 1, d, tj), lambda i, j, ie, ist, ir: (layer, ie[i], 0, n_j + jj(i, j, ir))),
            pl.BlockSpec((1, 1, 1, tj), lambda i, j, ie, ist, ir: (layer, ie[i], 0, jj(i, j, ir))),
            pl.BlockSpec((1, 1, 1, tj), lambda i, j, ie, ist, ir: (layer, ie[i], 0, n_j + jj(i, j, ir))),
            pl.BlockSpec((1, 1, tj, d), lambda i, j, ie, ist, ir: (layer, ie[i], jj(i, j, ir), 0)),
            pl.BlockSpec((1, 1, 1, d), lambda i, j, ie, ist, ir: (layer, ie[i], 0, 0)),
        ],
        out_specs=pl.BlockSpec(memory_space=pl.ANY),
        scratch_shapes=[
            pltpu.SMEM((_MOE_CAP // LANES, 1, LANES), jnp.int32),
            pltpu.VMEM((_MOE_CAP, d), F32),
            pltpu.VMEM((_MOE_CAP, d), F32),
            pltpu.VMEM((_MOE_CAP, d), BF16),
            pltpu.VMEM((d, tj), BF16), pltpu.VMEM((d, tj), BF16), pltpu.VMEM((tj, d), BF16),
            pltpu.SemaphoreType.DMA(()), pltpu.SemaphoreType.DMA(()), pltpu.SemaphoreType.DMA(()),
        ],
    )
    b1r = b1.reshape(depth, N_EXPERTS, 1, 2 * D_EXPERT)
    return pl.pallas_call(
        _moe_kernel,
        grid_spec=grid_spec,
        out_shape=jax.ShapeDtypeStruct((n_rows, d), F32),
        compiler_params=_cparams(("arbitrary", "arbitrary"), 58),
        name="expert_ffn",
    )(item_e, item_start, item_rows, row_tok, h2, w1, w1, b1r, b1r, w2, b2.reshape(depth, N_EXPERTS, 1, d))


def _combine_kernel(pos_hbm, y_hbm, x1_ref, tw_ref, gate_ref, fg_ref, o_ref, idx_smem, gbuf, sem_idx, sem_g,
                    *, final_norm):
    i = pl.program_id(0)
    n_tiles = pl.num_programs(0)
    tt = _CMB_TT
    n_idx = tt * TOP_K
    idx_lines = n_idx // LANES
    tok_per_trip = _ROW_UNROLL // TOP_K

    def gather_start(tile, slot):
        cp = pltpu.make_async_copy(pos_hbm.at[pl.ds(tile * idx_lines, idx_lines)], idx_smem, sem_idx)
        cp.start()
        cp.wait()
        _issue_rows(n_idx, idx_smem,
                    lambda r, g, u: _row_copy(y_hbm, r, gbuf.at[slot],
                                              (u % TOP_K) * tt + g * tok_per_trip + u // TOP_K, sem_g.at[slot]))

    @pl.when(i == 0)
    def _():
        gather_start(0, 0)

    @pl.when(i + 1 < n_tiles)
    def _():
        gather_start(i + 1, (i + 1) % 2)

    slot = i % 2
    pltpu.make_async_copy(y_hbm.at[pl.ds(0, n_idx)], gbuf.at[slot], sem_g.at[slot]).wait()
    tw = tw_ref[...]
    mix = tw[:, 0:1] * gbuf[slot, 0:tt, :]
    for k in range(1, TOP_K):
        mix = mix + tw[:, k:k + 1] * gbuf[slot, k * tt:(k + 1) * tt, :]
    out = x1_ref[...] + gate_ref[...] * mix
    if final_norm:
        out = _rms(out, fg_ref[...])
    o_ref[...] = out


def _combine(pos, y_sorted, x1, top_w, gate2, final_g):
    s, d = x1.shape
    tt = _CMB_TT
    final_norm = final_g is not None
    fg = final_g if final_norm else jnp.ones((1, d), F32)
    row = lambda i: (i, 0)
    return pl.pallas_call(
        functools.partial(_combine_kernel, final_norm=final_norm),
        grid=(s // tt,),
        in_specs=[
            pl.BlockSpec(memory_space=pl.ANY), pl.BlockSpec(memory_space=pl.ANY),
            pl.BlockSpec((tt, d), row), pl.BlockSpec((tt, LANES), row), _const_spec((1, d)), _const_spec((1, d)),
        ],
        out_specs=pl.BlockSpec((tt, d), row),
        out_shape=jax.ShapeDtypeStruct((s, d), F32),
        scratch_shapes=[
            pltpu.SMEM((tt * TOP_K // LANES, 1, LANES), jnp.int32),
            pltpu.VMEM((2, TOP_K * tt, d), F32),
            pltpu.SemaphoreType.DMA(()), pltpu.SemaphoreType.DMA((2,)),
        ],
        compiler_params=_cparams(("arbitrary",), 40),
        name="expert_combine",
    )(pos.reshape(s * TOP_K // LANES, 1, LANES), y_sorted, x1, top_w, gate2, fg)


def _moe_layer(h2, top_idx, top_w, x1, gate2, w1, b1, w2, b2, layer, final_g):
    dest, row_tok, item_e, item_start, item_rows, n_rows, n_items = _routing_tables(top_idx)
    y_sorted = _moe_ffn(h2, row_tok, item_e, item_start, item_rows, n_rows, n_items, w1, b1, w2, b2, layer)
    return _combine(dest, y_sorted, x1, top_w, gate2, final_g)


def kernel(x, c, ada_w, ada_b, norm1_g, w_in, sgu_ln_g, sgu_ln_b, sgu_w, sgu_b, conv_dw_w, conv_dw_b, conv_ln_g,
           conv_ln_b, conv_pw_w, conv_pw_b, gla_w2, gla_b2, gla_norm_g, merge_g, w_out, norm2_g, router_w, router_b,
           moe_w1, moe_b1, moe_w2, moe_b2, final_g):
    bsz, s, d = x.shape
    assert bsz == 1 and d == D_MODEL
    depth = ada_w.shape[0]
    xs = x.reshape(s, d)
    mod = _modulation(c, ada_w, ada_b)
    row = lambda v: v.reshape(1, -1)
    for l in range(depth):
        shift1, scale1, gate1, shift2, scale2, gate2 = [mod[l, :, k * d:(k + 1) * d] for k in range(6)]
        mg = merge_g[l]
        z_a, z_b, z_c, z_d = _in_projection(xs, row(norm1_g[l]), scale1, shift1, _relayout_w_in(w_in[l]))
        y_a = _spatial_gating(z_a, row(sgu_ln_g[l]), row(sgu_ln_b[l]), sgu_w[l], sgu_b[l], row(mg[:GROUP_W]))
        y_b = _conv_module(z_b, conv_dw_w[l], row(conv_dw_b[l]), row(conv_ln_g[l]), row(conv_ln_b[l]),
                           conv_pw_w[l], row(conv_pw_b[l]), row(mg[GROUP_W:2 * GROUP_W]))
        y_c = _gla(z_c, gla_w2[l], row(gla_b2[l]), row(gla_norm_g[l]), row(mg[2 * GROUP_W:3 * GROUP_W]))
        y_d = _stick_breaking(z_d, row(mg[3 * GROUP_W:]))
        x1, h2, top_idx, top_w = _out_projection(y_a, y_b, y_c, y_d, w_out[l].astype(BF16), xs, gate1,
                                                 row(norm2_g[l]), scale2, shift2, router_w[l], row(router_b[l]))
        xs = _moe_layer(h2, top_idx, top_w, x1, gate2, moe_w1, moe_b1, moe_w2, moe_b2, l,
                        row(final_g) if l == depth - 1 else None)
    return xs.reshape(bsz, s, d)
```
